```python
import math
import jax, jax.numpy as jnp
from jax import lax
import numpy as np

D_MODEL = 2048
BATCH = 2
SEQ = 4096
DEPTH = 1
DEC_BATCH = 128
DEC_SEQ = 8
PAST_LEN = 2048
PAGE_SIZE = 128

C_CONV = D_MODEL // 2
CONV_W = 31
N_HEADS = 8
DH = 64
DV = 2 * DH
A_WIDTH = N_HEADS * DV
N_XHEADS = 4
XDH = D_MODEL // 8
X_WIDTH = N_XHEADS * XDH
N_MEM = 256
N_BRANCH = 3
D_FF = 4 * D_MODEL
N_BUCKETS = 32
MAX_DIST = 128
Q_BLOCK = 128
EPS = 1e-6

O_Q = 2 * C_CONV
O_K = O_Q + N_HEADS * 2 * DH
O_V = O_K + N_HEADS * 2 * DH
O_XQ = O_V + A_WIDTH
O_GATE = O_XQ + X_WIDTH
IN_WIDTH = O_GATE + N_BRANCH * D_MODEL

kernel_name = "gated_conformer_diffattn_memory_decoder_step"


def rmsnorm(x, g):
    xf = x.astype(jnp.float32)
    y = xf * lax.rsqrt(jnp.mean(xf * xf, axis=-1, keepdims=True) + EPS) * g.astype(jnp.float32)
    return y.astype(x.dtype)


def layernorm(x, g, b):
    xf = x.astype(jnp.float32)
    mu = jnp.mean(xf, axis=-1, keepdims=True)
    var = jnp.mean(jnp.square(xf - mu), axis=-1, keepdims=True)
    y = (xf - mu) * lax.rsqrt(var + EPS) * g.astype(jnp.float32) + b.astype(jnp.float32)
    return y.astype(x.dtype)


def rel_bucket(dist):
    n = jnp.maximum(dist, 0)
    max_exact = N_BUCKETS // 2
    nf = jnp.maximum(n, 1).astype(jnp.float32)
    large = max_exact + (jnp.log(nf / max_exact) / math.log(MAX_DIST / max_exact)
                         * (N_BUCKETS - max_exact)).astype(jnp.int32)
    large = jnp.minimum(large, N_BUCKETS - 1)
    return jnp.where(n < max_exact, n, large)


def rel_bias(q_pos, k_pos, table):
    dist = q_pos[:, None] - k_pos[None, :]
    bias = table[rel_bucket(dist)].astype(jnp.float32)
    return jnp.transpose(bias, (2, 0, 1)), dist >= 0


def diff_lambda(lq1, lk1, lq2, lk2, lam_init):
    f = lambda a, b: jnp.sum(a.astype(jnp.float32) * b.astype(jnp.float32))
    return jnp.exp(f(lq1, lk1)) - jnp.exp(f(lq2, lk2)) + lam_init


def diff_combine(l1, l2, mask, lam):
    l1 = jnp.where(mask, l1, -jnp.inf)
    l2 = jnp.where(mask, l2, -jnp.inf)
    return jax.nn.softmax(l1, axis=-1) - lam * jax.nn.softmax(l2, axis=-1)


def qk_logits(qx, kx):
    return jnp.einsum('bqhd,bkhd->bhqk', qx, kx).astype(jnp.float32) * (DH ** -0.5)


def diff_attn_prompt(q, k, v, table, lam):
    B, S = q.shape[0], q.shape[1]
    nb = S // Q_BLOCK
    qb = q.reshape(B, nb, Q_BLOCK, N_HEADS, 2 * DH).transpose(1, 0, 2, 3, 4)
    k1, k2 = k[..., :DH], k[..., DH:]
    k_pos = jnp.arange(S)

    def one_block(args):
        i, qi = args
        q_pos = i * Q_BLOCK + jnp.arange(Q_BLOCK)
        bias, mask = rel_bias(q_pos, k_pos, table)
        l1 = qk_logits(qi[..., :DH], k1) + bias[None]
        l2 = qk_logits(qi[..., DH:], k2) + bias[None]
        w = diff_combine(l1, l2, mask, lam).astype(v.dtype)
        return jnp.einsum('bhqk,bkhd->bqhd', w, v)

    o = lax.map(one_block, (jnp.arange(nb), qb))
    return o.transpose(1, 0, 2, 3, 4).reshape(B, S, N_HEADS, DV)


def diff_attn_sample(q, k_new, v_new, k_past, v_past, table, lam):
    P, T = k_past.shape[1], q.shape[1]
    bias, mask = rel_bias(P + jnp.arange(T), jnp.arange(P + T), table)

    def logits(lo, hi):
        s = jnp.concatenate([qk_logits(q[..., lo:hi], k_past[..., lo:hi]),
                             qk_logits(q[..., lo:hi], k_new[..., lo:hi])], axis=-1)
        return s + bias[None]

    w = diff_combine(logits(0, DH), logits(DH, 2 * DH), mask, lam).astype(v_new.dtype)
    return (jnp.einsum('bhqk,bkhd->bqhd', w[..., :P], v_past)
            + jnp.einsum('bhqk,bkhd->bqhd', w[..., P:], v_new))


def head_out(o, g_sub, lam_init):
    y = rmsnorm(o, g_sub) * (1.0 - lam_init)
    return y.reshape(o.shape[0], o.shape[1], A_WIDTH)


def conv_branch(a, prefix, w_dw, b_dw, ln_g, ln_b, w_out):
    xp = jnp.concatenate([prefix, a], axis=1)
    y = lax.conv_general_dilated(xp, w_dw[:, None, :], window_strides=(1,), padding='VALID',
                                 dimension_numbers=('NWC', 'WIO', 'NWC'),
                                 feature_group_count=C_CONV) + b_dw
    y = jax.nn.silu(layernorm(y, ln_g, ln_b))
    return y @ w_out, xp[:, -(CONV_W - 1):]


def mem_kv(mem, g_mem, w_k, w_v):
    m = rmsnorm(mem, g_mem)
    N, M = mem.shape[0], mem.shape[1]
    return (m @ w_k).reshape(N, M, N_XHEADS, XDH), (m @ w_v).reshape(N, M, N_XHEADS, XDH)


def cross_attend(qc, mk, mv):
    s = jnp.einsum('nthd,nmhd->nhtm', qc, mk).astype(jnp.float32) * (XDH ** -0.5)
    p = jax.nn.softmax(s, axis=-1).astype(mv.dtype)
    o = jnp.einsum('nhtm,nmhd->nthd', p, mv)
    return o.reshape(qc.shape[0], qc.shape[1], X_WIDTH)


def split_in(x, g_pre, w_in):
    z = rmsnorm(x, g_pre) @ w_in
    N, T = z.shape[0], z.shape[1]
    a = z[..., :C_CONV] * jax.nn.sigmoid(z[..., C_CONV:O_Q])
    q = z[..., O_Q:O_K].reshape(N, T, N_HEADS, 2 * DH)
    k = z[..., O_K:O_V].reshape(N, T, N_HEADS, 2 * DH)
    v = z[..., O_V:O_XQ].reshape(N, T, N_HEADS, DV)
    qc = z[..., O_XQ:O_GATE].reshape(N, T, N_XHEADS, XDH)
    gates = jax.nn.sigmoid(z[..., O_GATE:].reshape(N, T, N_BRANCH, D_MODEL))
    return a, q, k, v, qc, gates


def layer_tail(x, conv_o, attn_o, cross_o, gates, w_o, g_mix_post, g_ffn_pre, w_up, w_down, g_ffn_post):
    merged = gates[..., 0, :] * conv_o + gates[..., 1, :] * attn_o + gates[..., 2, :] * cross_o
    x = x + rmsnorm(merged @ w_o, g_mix_post)
    u = jnp.square(jax.nn.relu(rmsnorm(x, g_ffn_pre) @ w_up))
    return x + rmsnorm(u @ w_down, g_ffn_post)


def setup_inputs(seed: int = 0) -> dict:
    key = jax.random.key(seed)
    ks = iter(jax.random.split(key, 40))
    f32 = jnp.float32
    n_pages = PAST_LEN // PAGE_SIZE
    n_pool = (DEC_BATCH * n_pages * 5) // 4
    nrm = lambda shape, s=1.0: jax.random.normal(next(ks), shape, f32) * s
    gain = lambda shape: 1.0 + 0.05 * jax.random.normal(next(ks), shape, f32)
    inp = {}
    inp['x_prompt'] = nrm((BATCH, SEQ, D_MODEL))
    inp['x_sample'] = nrm((DEC_BATCH, DEC_SEQ, D_MODEL))
    inp['mem_prompt'] = nrm((BATCH, N_MEM, D_MODEL))
    inp['cache_k'] = nrm((DEPTH, n_pool, PAGE_SIZE, N_HEADS, 2 * DH))
    inp['cache_v'] = nrm((DEPTH, n_pool, PAGE_SIZE, N_HEADS, DV))
    inp['cache_conv'] = nrm((DEPTH, DEC_BATCH, CONV_W - 1, C_CONV), 0.5)
    inp['cache_mem_k'] = nrm((DEPTH, DEC_BATCH, N_MEM, N_XHEADS, XDH))
    inp['cache_mem_v'] = nrm((DEPTH, DEC_BATCH, N_MEM, N_XHEADS, XDH))
    perm = jax.random.permutation(next(ks), n_pool)[: DEC_BATCH * n_pages]
    inp['page_table'] = perm.reshape(DEC_BATCH, n_pages).astype(jnp.int32)
    inp['rel_table'] = nrm((N_BUCKETS, N_HEADS), 0.5)
    inp['g_mix_pre'] = gain((DEPTH, D_MODEL))
    inp['w_in'] = nrm((DEPTH, D_MODEL, IN_WIDTH), D_MODEL ** -0.5)
    inp['conv_dw'] = nrm((DEPTH, CONV_W, C_CONV), CONV_W ** -0.5)
    inp['conv_db'] = nrm((DEPTH, C_CONV), 0.02)
    inp['conv_ln_g'] = gain((DEPTH, C_CONV))
    inp['conv_ln_b'] = nrm((DEPTH, C_CONV), 0.02)
    inp['w_conv_out'] = nrm((DEPTH, C_CONV, D_MODEL), C_CONV ** -0.5)
    inp['lambda_q1'] = nrm((DEPTH, DH), 0.1)
    inp['lambda_k1'] = nrm((DEPTH, DH), 0.1)
    inp['lambda_q2'] = nrm((DEPTH, DH), 0.1)
    inp['lambda_k2'] = nrm((DEPTH, DH), 0.1)
    inp['g_attn_sub'] = gain((DEPTH, DV))
    inp['w_attn_out'] = nrm((DEPTH, A_WIDTH, D_MODEL), A_WIDTH ** -0.5)
    inp['g_mem'] = gain((DEPTH, D_MODEL))
    inp['w_mem_k'] = nrm((DEPTH, D_MODEL, X_WIDTH), D_MODEL ** -0.5)
    inp['w_mem_v'] = nrm((DEPTH, D_MODEL, X_WIDTH), D_MODEL ** -0.5)
    inp['w_cross_out'] = nrm((DEPTH, X_WIDTH, D_MODEL), X_WIDTH ** -0.5)
    inp['w_o'] = nrm((DEPTH, D_MODEL, D_MODEL), D_MODEL ** -0.5)
    inp['g_mix_post'] = gain((DEPTH, D_MODEL))
    inp['g_ffn_pre'] = gain((DEPTH, D_MODEL))
    inp['w_up'] = nrm((DEPTH, D_MODEL, D_FF), D_MODEL ** -0.5)
    inp['w_down'] = nrm((DEPTH, D_FF, D_MODEL), D_FF ** -0.5)
    inp['g_ffn_post'] = gain((DEPTH, D_MODEL))
    return inp


def reference(x_prompt, x_sample, mem_prompt, cache_k, cache_v, cache_conv, cache_mem_k, cache_mem_v,
              page_table, rel_table, g_mix_pre, w_in, conv_dw, conv_db, conv_ln_g, conv_ln_b, w_conv_out,
              lambda_q1, lambda_k1, lambda_q2, lambda_k2, g_attn_sub, w_attn_out, g_mem, w_mem_k, w_mem_v,
              w_cross_out, w_o, g_mix_post, g_ffn_pre, w_up, w_down, g_ffn_post):
    xp, xs = x_prompt, x_sample
    n_seq, n_pages = page_table.shape
    past = n_pages * cache_k.shape[2]
    kp_l, vp_l, cp_l, mkp_l, mvp_l, ks_l, vs_l, cs_l = [], [], [], [], [], [], [], []
    for l in range(DEPTH):
        lam_init = 0.8 - 0.6 * math.exp(-0.3 * l)
        lam = diff_lambda(lambda_q1[l], lambda_k1[l], lambda_q2[l], lambda_k2[l], lam_init)
        tail = (w_o[l], g_mix_post[l], g_ffn_pre[l], w_up[l], w_down[l], g_ffn_post[l])
        conv_p = (conv_dw[l], conv_db[l], conv_ln_g[l], conv_ln_b[l], w_conv_out[l])

        a, q, k, v, qc, gates = split_in(xp, g_mix_pre[l], w_in[l])
        zeros = jnp.zeros((a.shape[0], CONV_W - 1, C_CONV), a.dtype)
        conv_o, conv_tail = conv_branch(a, zeros, *conv_p)
        att = diff_attn_prompt(q, k, v, rel_table, lam)
        attn_o = head_out(att, g_attn_sub[l], lam_init) @ w_attn_out[l]
        mk, mv = mem_kv(mem_prompt, g_mem[l], w_mem_k[l], w_mem_v[l])
        cross_o = cross_attend(qc, mk, mv) @ w_cross_out[l]
        xp = layer_tail(xp, conv_o, attn_o, cross_o, gates, *tail)
        kp_l.append(k); vp_l.append(v); cp_l.append(conv_tail); mkp_l.append(mk); mvp_l.append(mv)

        a, q, k, v, qc, gates = split_in(xs, g_mix_pre[l], w_in[l])
        conv_o, conv_tail = conv_branch(a, cache_conv[l], *conv_p)
        k_past = cache_k[l][page_table].reshape(n_seq, past, N_HEADS, 2 * DH)
        v_past = cache_v[l][page_table].reshape(n_seq, past, N_HEADS, DV)
        att = diff_attn_sample(q, k, v, k_past, v_past, rel_table, lam)
        attn_o = head_out(att, g_attn_sub[l], lam_init) @ w_attn_out[l]
        cross_o = cross_attend(qc, cache_mem_k[l], cache_mem_v[l]) @ w_cross_out[l]
        xs = layer_tail(xs, conv_o, attn_o, cross_o, gates, *tail)
        ks_l.append(k); vs_l.append(v); cs_l.append(conv_tail)

    return (xp, xs, jnp.stack(kp_l), jnp.stack(vp_l), jnp.stack(cp_l), jnp.stack(mkp_l), jnp.stack(mvp_l),
            jnp.stack(ks_l), jnp.stack(vs_l), jnp.stack(cs_l))
```

```python
import functools
import math

import jax
import jax.numpy as jnp
from jax import lax
from jax.experimental import pallas as pl
from jax.experimental.pallas import tpu as pltpu

F32 = jnp.float32
BF16 = jnp.bfloat16

D_MODEL = 2048
C_CONV = D_MODEL // 2
CONV_W = 31
N_HEADS = 8
DH = 64
DV = 2 * DH
A_WIDTH = N_HEADS * DV
N_XHEADS = 4
XDH = D_MODEL // 8
X_WIDTH = N_XHEADS * XDH
N_BRANCH = 3
N_BUCKETS = 32
MAX_EXACT = N_BUCKETS // 2
MAX_DIST = 128
EPS = 1e-6
NEG = -1e30

V7X_VMEM_LIMIT_BYTES = 56 * 1024 * 1024
SUBLANES = 8
CONV_HALO = 32

TN_IN = 512
TM_ROWS = 512
TF_FFN = 512
T_ATT = 256
TT_CONV = 256
RC_CONV = 32
TS_CROSS = 512
NB_SAMPLE = 8


def _cparams(*sem):
    return pltpu.CompilerParams(dimension_semantics=sem, vmem_limit_bytes=V7X_VMEM_LIMIT_BYTES)


def _rms(x, g):
    return x * lax.rsqrt(jnp.mean(x * x, axis=-1, keepdims=True) + EPS) * g


_NT_GLU = 2 * C_CONV // TN_IN
_NT_HEAD = A_WIDTH // TN_IN
_J_Q = _NT_GLU
_J_K = _J_Q + _NT_HEAD
_J_V = _J_K + _NT_HEAD
_J_QC = _J_V + _NT_HEAD
_J_GATE = _J_QC + X_WIDTH // TN_IN
_NT_GATE = N_BRANCH * D_MODEL // TN_IN
_NT_IN = _J_GATE + _NT_GATE


def _inproj_kernel(x_ref, g_ref, w_ref, a_ref, q_ref, k_ref, kb_ref, v_ref, vb_ref, qc_ref, gt_ref, h_ref):
    j = pl.program_id(1)

    @pl.when(j == 0)
    def _():
        h_ref[...] = _rms(x_ref[...], g_ref[...]).astype(BF16)

    z = jnp.dot(h_ref[...], w_ref[...], preferred_element_type=F32)
    half = TN_IN // 2

    @pl.when(j < _J_Q)
    def _():
        a_ref[...] = z[:, :half] * jax.nn.sigmoid(z[:, half:])

    @pl.when((j >= _J_Q) & (j < _J_K))
    def _():
        q_ref[...] = (z * DH ** -0.5).astype(BF16)

    @pl.when((j >= _J_K) & (j < _J_V))
    def _():
        k_ref[...] = z
        kb_ref[...] = z.astype(BF16)

    @pl.when((j >= _J_V) & (j < _J_QC))
    def _():
        v_ref[...] = z
        vb_ref[...] = z.astype(BF16)

    @pl.when((j >= _J_QC) & (j < _J_GATE))
    def _():
        qc_ref[...] = (z * XDH ** -0.5).astype(BF16)

    @pl.when(j >= _J_GATE)
    def _():
        gt_ref[...] = jax.nn.sigmoid(z).astype(BF16)


def _section_map(j0, n):
    return lambda i, j: (i, jnp.clip(j - j0, 0, n - 1))


def _inproj(x2d, g, w_perm, tm):
    m = x2d.shape[0]
    half = TN_IN // 2
    outs = (
        (C_CONV, half, F32, 0, _NT_GLU),
        (A_WIDTH, TN_IN, BF16, _J_Q, _NT_HEAD),
        (A_WIDTH, TN_IN, F32, _J_K, _NT_HEAD),
        (A_WIDTH, TN_IN, BF16, _J_K, _NT_HEAD),
        (A_WIDTH, TN_IN, F32, _J_V, _NT_HEAD),
        (A_WIDTH, TN_IN, BF16, _J_V, _NT_HEAD),
        (X_WIDTH, TN_IN, BF16, _J_QC, X_WIDTH // TN_IN),
        (N_BRANCH * D_MODEL, TN_IN, BF16, _J_GATE, _NT_GATE),
    )
    return pl.pallas_call(
        _inproj_kernel,
        grid=(m // tm, _NT_IN),
        in_specs=[
            pl.BlockSpec((tm, D_MODEL), lambda i, j: (i, 0)),
            pl.BlockSpec((1, D_MODEL), lambda i, j: (0, 0)),
            pl.BlockSpec((D_MODEL, TN_IN), lambda i, j: (0, j)),
        ],
        out_specs=[pl.BlockSpec((tm, bw), _section_map(j0, n)) for (_, bw, _, j0, n) in outs],
        out_shape=[jax.ShapeDtypeStruct((m, w), dt) for (w, _, dt, _, _) in outs],
        scratch_shapes=[pltpu.VMEM((tm, D_MODEL), BF16)],
        compiler_params=_cparams("parallel", "arbitrary"),
        name="inproj",
    )(x2d, g, w_perm)


def _permute_w_in(w_in):
    half = TN_IN // 2
    glu = w_in[:, : 2 * C_CONV].reshape(D_MODEL, 2, C_CONV // half, half)
    glu = glu.transpose(0, 2, 1, 3).reshape(D_MODEL, 2 * C_CONV)
    return jnp.concatenate([glu, w_in[:, 2 * C_CONV:]], axis=1).astype(BF16)


def _norm_matmul_kernel(x_ref, g_ref, w_ref, o_ref, h_ref):
    @pl.when(pl.program_id(1) == 0)
    def _():
        h_ref[...] = _rms(x_ref[...], g_ref[...]).astype(BF16)

    o_ref[...] = jnp.dot(h_ref[...], w_ref[...], preferred_element_type=F32)


def _norm_matmul(x2d, g, w, tm, tn):
    m, k = x2d.shape
    n = w.shape[1]
    return pl.pallas_call(
        _norm_matmul_kernel,
        grid=(m // tm, n // tn),
        in_specs=[
            pl.BlockSpec((tm, k), lambda i, j: (i, 0)),
            pl.BlockSpec((1, k), lambda i, j: (0, 0)),
            pl.BlockSpec((k, tn), lambda i, j: (0, j)),
        ],
        out_specs=pl.BlockSpec((tm, tn), lambda i, j: (i, j)),
        out_shape=jax.ShapeDtypeStruct((m, n), F32),
        scratch_shapes=[pltpu.VMEM((tm, k), BF16)],
        compiler_params=_cparams("parallel", "arbitrary"),
        name="norm_matmul",
    )(x2d, g, w)


def _ln_swish(y, g, beta):
    mu = jnp.mean(y, axis=-1, keepdims=True)
    var = jnp.mean(jnp.square(y - mu), axis=-1, keepdims=True)
    yn = (y - mu) * lax.rsqrt(var + EPS) * g + beta
    return yn * jax.nn.sigmoid(yn)


def _conv_prompt_kernel(cur_ref, prev_ref, w_ref, b_ref, g_ref, beta_ref, y_ref, ext_ref, sh_ref):
    t = pl.program_id(1)
    tt = cur_ref.shape[1]
    ext_ref[0:CONV_HALO, :] = jnp.where(t == 0, 0.0, prev_ref[0])
    ext_ref[CONV_HALO:CONV_HALO + tt, :] = cur_ref[0]
    lead = CONV_HALO - (CONV_W - 1)
    for s in range(SUBLANES):
        rows = tt + SUBLANES * ((CONV_W - 1 - s) // SUBLANES)
        sh_ref[s, 0:rows, :] = ext_ref[lead + s:lead + s + rows, :]

    def chunk(c, carry):
        base = pl.multiple_of(c * RC_CONV, RC_CONV)
        acc = jnp.zeros((RC_CONV, C_CONV), F32)
        for j in range(CONV_W):
            a, s = divmod(j, SUBLANES)
            acc = acc + sh_ref[s, pl.ds(base + SUBLANES * a, RC_CONV), :] * w_ref[j:j + 1, :]
        y = _ln_swish(acc + b_ref[...], g_ref[...], beta_ref[...])
        y_ref[0, pl.ds(base, RC_CONV), :] = y.astype(y_ref.dtype)
        return carry

    lax.fori_loop(0, tt // RC_CONV, chunk, 0)


def _conv_prompt(a3, w_dw, b_dw, ln_g, ln_b):
    b, s, c = a3.shape
    tt = TT_CONV
    ratio = tt // CONV_HALO
    vec = pl.BlockSpec((1, c), lambda i, t: (0, 0))
    return pl.pallas_call(
        _conv_prompt_kernel,
        grid=(b, s // tt),
        in_specs=[
            pl.BlockSpec((1, tt, c), lambda i, t: (i, t, 0)),
            pl.BlockSpec((1, CONV_HALO, c), lambda i, t: (i, jnp.maximum(t * ratio - 1, 0), 0)),
            pl.BlockSpec((CONV_W, c), lambda i, t: (0, 0)),
            vec, vec, vec,
        ],
        out_specs=pl.BlockSpec((1, tt, c), lambda i, t: (i, t, 0)),
        out_shape=jax.ShapeDtypeStruct((b, s, c), BF16),
        scratch_shapes=[
            pltpu.VMEM((tt + CONV_HALO, c), F32),
            pltpu.VMEM((SUBLANES, tt + SUBLANES * ((CONV_W - 1) // SUBLANES), c), F32),
        ],
        compiler_params=_cparams("parallel", "arbitrary"),
        name="conv_prompt",
    )(a3, a3, w_dw, b_dw, ln_g, ln_b)


def _conv_sample_kernel(cache_ref, a_ref, w_ref, b_ref, g_ref, beta_ref, y_ref, ext_ref):
    nb, t_new = a_ref.shape[0], a_ref.shape[1]
    n_pre = cache_ref.shape[1]
    ext_ref[:, 0:n_pre, :] = cache_ref[...]
    ext_ref[:, n_pre:n_pre + t_new, :] = a_ref[...]

    def one(n, carry):
        acc = jnp.zeros((t_new, C_CONV), F32)
        for j in range(CONV_W):
            acc = acc + ext_ref[n, j:j + t_new, :] * w_ref[j:j + 1, :]
        y_ref[n] = _ln_swish(acc + b_ref[...], g_ref[...], beta_ref[...])
        return carry

    lax.fori_loop(0, nb, one, 0)


def _conv_sample(cache, a3, w_dw, b_dw, ln_g, ln_b):
    n, t_new, c = a3.shape
    n_pre = cache.shape[1]
    nb = NB_SAMPLE
    vec = pl.BlockSpec((1, c), lambda i: (0, 0))
    return pl.pallas_call(
        _conv_sample_kernel,
        grid=(n // nb,),
        in_specs=[
            pl.BlockSpec((nb, n_pre, c), lambda i: (i, 0, 0)),
            pl.BlockSpec((nb, t_new, c), lambda i: (i, 0, 0)),
            pl.BlockSpec((CONV_W, c), lambda i: (0, 0)),
            vec, vec, vec,
        ],
        out_specs=pl.BlockSpec((nb, t_new, c), lambda i: (i, 0, 0)),
        out_shape=jax.ShapeDtypeStruct((n, t_new, c), F32),
        scratch_shapes=[pltpu.VMEM((nb, n_pre + t_new + 2, c), F32)],
        compiler_params=_cparams("parallel"),
        name="conv_sample",
    )(cache, a3, w_dw, b_dw, ln_g, ln_b)


def _shifted_bias(d, tab_ref, h):
    n = jnp.maximum(d, 0)
    nf = jnp.maximum(n, 1).astype(F32)
    large = MAX_EXACT + (jnp.log(nf / MAX_EXACT) / math.log(MAX_DIST / MAX_EXACT)
                         * (N_BUCKETS - MAX_EXACT)).astype(jnp.int32)
    large = jnp.minimum(large, N_BUCKETS - 1)
    bucket = jnp.where(n < MAX_EXACT, n, large)
    far = tab_ref[N_BUCKETS - 1, h]
    out = jnp.zeros(d.shape, F32)
    for b in range(N_BUCKETS - 1):
        out = jnp.where(bucket == b, tab_ref[b, h] - far, out)
    return jnp.where(d >= 0, out, NEG)


def _prompt_bias_kernel(tab_ref, o_ref):
    h = pl.program_id(0)
    t = o_ref.shape[-1]
    d = lax.broadcasted_iota(jnp.int32, (t, t), 0) - lax.broadcasted_iota(jnp.int32, (t, t), 1)
    o_ref[0, 0] = _shifted_bias(d, tab_ref, h)
    o_ref[0, 1] = _shifted_bias(d + t, tab_ref, h)


def _prompt_bias(rel_table, t):
    return pl.pallas_call(
        _prompt_bias_kernel,
        grid=(N_HEADS,),
        in_specs=[pl.BlockSpec(memory_space=pltpu.SMEM)],
        out_specs=pl.BlockSpec((1, 2, t, t), lambda h: (h, 0, 0, 0)),
        out_shape=jax.ShapeDtypeStruct((N_HEADS, 2, t, t), F32),
        compiler_params=_cparams("arbitrary"),
        name="prompt_bias",
    )(rel_table)


def _sample_bias_kernel(tab_ref, past_ref, new_ref, *, page, t_new):
    def fill(ref, off):
        shape = ref.shape
        col = lax.broadcasted_iota(jnp.int32, shape, 1)
        row = lax.broadcasted_iota(jnp.int32, shape, 0)
        d = off + col % t_new - row
        head = col // (2 * t_new)
        out = jnp.zeros(shape, F32)
        for h in range(N_HEADS):
            out = jnp.where(head == h, _shifted_bias(d, tab_ref, h), out)
        ref[...] = out

    fill(past_ref, page)
    fill(new_ref, 0)


def _sample_bias(rel_table, page, t_new):
    ncol = N_HEADS * 2 * t_new
    return pl.pallas_call(
        functools.partial(_sample_bias_kernel, page=page, t_new=t_new),
        in_specs=[pl.BlockSpec(memory_space=pltpu.SMEM)],
        out_shape=[jax.ShapeDtypeStruct((page, ncol), F32), jax.ShapeDtypeStruct((t_new, ncol), F32)],
        name="sample_bias",
    )(rel_table)


def _diff_lambda(lq1, lk1, lq2, lk2, lam_init):
    s1 = jnp.sum(lq1 * lk1, axis=-1, keepdims=True)
    s2 = jnp.sum(lq2 * lk2, axis=-1, keepdims=True)
    return jnp.exp(s1) - jnp.exp(s2) + lam_init


def _head_out(o, gsub, lam_init):
    return _rms(o, gsub) * (1.0 - lam_init)


def _flash_kernel(q_ref, k_ref, v_ref, bias_ref, lq1_ref, lk1_ref, lq2_ref, lk2_ref, gsub_ref, o_ref,
                  qs_ref, m_ref, l_ref, acc_ref, *, lam_init):
    qi = pl.program_id(2)
    t = q_ref.shape[1]
    q = q_ref[0]
    lane = lax.broadcasted_iota(jnp.int32, q.shape, 1)
    zero = jnp.zeros_like(q)
    qs_ref[0:t, :] = jnp.where(lane < DH, q, zero)
    qs_ref[t:2 * t, :] = jnp.where(lane >= DH, q, zero)
    m_ref[...] = jnp.full(m_ref.shape, NEG, F32)
    l_ref[...] = jnp.zeros(l_ref.shape, F32)
    acc_ref[...] = jnp.zeros(acc_ref.shape, F32)

    def tile(ki, bias):
        start = pl.multiple_of(ki * t, t)
        k = k_ref[0, pl.ds(start, t), :]
        v = v_ref[0, pl.ds(start, t), :]
        s = lax.dot_general(qs_ref[...], k, (((1,), (1,)), ((), ())), preferred_element_type=F32)
        if bias is not None:
            s = s + jnp.concatenate([bias, bias], axis=0)
        m_prev = m_ref[...]
        m_new = jnp.maximum(m_prev, jnp.max(s, axis=-1, keepdims=True))
        alpha = jnp.exp(m_prev - m_new)
        p = jnp.exp(s - m_new)
        l_ref[...] = alpha * l_ref[...] + jnp.sum(p, axis=-1, keepdims=True)
        acc_ref[...] = alpha * acc_ref[...] + jnp.dot(p.astype(BF16), v, preferred_element_type=F32)
        m_ref[...] = m_new

    def far(ki, carry):
        tile(ki, None)
        return carry

    lax.fori_loop(0, jnp.maximum(qi - 1, 0), far, 0)

    @pl.when(qi >= 1)
    def _():
        tile(qi - 1, bias_ref[0, 1])

    tile(qi, bias_ref[0, 0])

    lam = _diff_lambda(lq1_ref[...], lk1_ref[...], lq2_ref[...], lk2_ref[...], lam_init)
    o1 = acc_ref[0:t, :] / l_ref[0:t, :]
    o2 = acc_ref[t:2 * t, :] / l_ref[t:2 * t, :]
    o_ref[0] = _head_out(o1 - lam * o2, gsub_ref[...], lam_init).astype(o_ref.dtype)


def _flash_prompt(qb, kb, vb, bias, lam_p, gsub, lam_init):
    b, s, _ = qb.shape
    t = T_ATT
    assert t >= MAX_DIST and s % t == 0
    vec = pl.BlockSpec((1, DH), lambda i, h, qi: (0, 0))
    return pl.pallas_call(
        functools.partial(_flash_kernel, lam_init=lam_init),
        grid=(b, N_HEADS, s // t),
        in_specs=[
            pl.BlockSpec((1, t, DV), lambda i, h, qi: (i, qi, h)),
            pl.BlockSpec((1, s, DV), lambda i, h, qi: (i, 0, h)),
            pl.BlockSpec((1, s, DV), lambda i, h, qi: (i, 0, h)),
            pl.BlockSpec((1, 2, t, t), lambda i, h, qi: (h, 0, 0, 0)),
            vec, vec, vec, vec,
            pl.BlockSpec((1, DV), lambda i, h, qi: (0, 0)),
        ],
        out_specs=pl.BlockSpec((1, t, DV), lambda i, h, qi: (i, qi, h)),
        out_shape=jax.ShapeDtypeStruct((b, s, A_WIDTH), BF16),
        scratch_shapes=[
            pltpu.VMEM((2 * t, DV), BF16),
            pltpu.VMEM((2 * t, 1), F32),
            pltpu.VMEM((2 * t, 1), F32),
            pltpu.VMEM((2 * t, DV), F32),
        ],
        compiler_params=_cparams("parallel", "parallel", "arbitrary"),
        name="flash_prompt",
    )(qb, kb, vb, bias, *lam_p, gsub)


def _sample_attn_kernel(pt_ref, q_ref, kn_ref, vn_ref, *rest, n_pages, lam_init):
    del pt_ref
    kp_refs = rest[:n_pages]
    vp_refs = rest[n_pages:2 * n_pages]
    (bpast_ref, bnew_ref, lq1_ref, lk1_ref, lq2_ref, lk2_ref, gsub_ref, o_ref,
     kall_ref, vall_ref, s_ref) = rest[2 * n_pages:]
    t_new = q_ref.shape[1]
    page = kp_refs[0].shape[1]
    ncol = N_HEADS * 2 * t_new
    nt = (((1,), (1,)), ((), ()))
    tn = (((0,), (0,)), ((), ()))

    qt = jnp.tile(q_ref[0].astype(F32), (N_HEADS * 2, 1))
    rg = lax.broadcasted_iota(jnp.int32, qt.shape, 0) // t_new
    cg = lax.broadcasted_iota(jnp.int32, qt.shape, 1) // DH
    wq = jnp.where(rg == cg, qt, jnp.zeros_like(qt)).astype(BF16)

    for p in range(n_pages):
        kall_ref[p * page:(p + 1) * page, :] = kp_refs[p][0].astype(BF16)
        vall_ref[p * page:(p + 1) * page, :] = vp_refs[p][0].astype(BF16)

    past = n_pages * page
    s_ref[...] = lax.dot_general(kall_ref[...], wq, nt, preferred_element_type=F32)
    s_ref[past - page:past, :] = s_ref[past - page:past, :] + bpast_ref[...]
    s_new = lax.dot_general(kn_ref[0].astype(BF16), wq, nt, preferred_element_type=F32) + bnew_ref[...]

    s_past = s_ref[...]
    m = jnp.maximum(jnp.max(s_past, axis=0, keepdims=True), jnp.max(s_new, axis=0, keepdims=True))
    e_past = jnp.exp(s_past - m)
    e_new = jnp.exp(s_new - m)
    denom = jnp.sum(e_past, axis=0, keepdims=True) + jnp.sum(e_new, axis=0, keepdims=True)
    lam = _diff_lambda(lq1_ref[...], lk1_ref[...], lq2_ref[...], lk2_ref[...], lam_init)
    col = lax.broadcasted_iota(jnp.int32, (1, ncol), 1)
    sign = jnp.where(col % (2 * t_new) < t_new, 1.0, -lam)
    scale = sign / denom
    w_past = (e_past * scale).astype(BF16)
    w_new = (e_new * scale).astype(BF16)

    full = lax.dot_general(w_past, vall_ref[...], tn, preferred_element_type=F32)
    full = full + lax.dot_general(w_new, vn_ref[0].astype(BF16), tn, preferred_element_type=F32)
    for h in range(N_HEADS):
        r0 = h * 2 * t_new
        blk = full[r0:r0 + 2 * t_new, h * DV:(h + 1) * DV]
        o = blk[0:t_new] + blk[t_new:2 * t_new]
        o_ref[0, :, h * DV:(h + 1) * DV] = _head_out(o, gsub_ref[...], lam_init)


def _sample_attn(page_table, q3, k3, v3, cache_k, cache_v, bias_past, bias_new, lam_p, gsub, lam_init):
    n, t_new, _ = q3.shape
    n_pages = page_table.shape[1]
    page = cache_k.shape[1]
    assert page >= MAX_DIST
    ncol = N_HEADS * 2 * t_new
    tok = pl.BlockSpec((1, t_new, A_WIDTH), lambda i, pt: (i, 0, 0))
    pages = [pl.BlockSpec((1, page, A_WIDTH), functools.partial(lambda i, pt, p: (pt[i, p], 0, 0), p=p))
             for p in range(n_pages)]
    vec = pl.BlockSpec((1, DH), lambda i, pt: (0, 0))
    grid_spec = pltpu.PrefetchScalarGridSpec(
        num_scalar_prefetch=1,
        grid=(n,),
        in_specs=[tok, tok, tok] + pages + pages + [
            pl.BlockSpec((page, ncol), lambda i, pt: (0, 0)),
            pl.BlockSpec((t_new, ncol), lambda i, pt: (0, 0)),
            vec, vec, vec, vec,
            pl.BlockSpec((1, DV), lambda i, pt: (0, 0)),
        ],
        out_specs=pl.BlockSpec((1, t_new, A_WIDTH), lambda i, pt: (i, 0, 0)),
        scratch_shapes=[
            pltpu.VMEM((n_pages * page, A_WIDTH), BF16),
            pltpu.VMEM((n_pages * page, A_WIDTH), BF16),
            pltpu.VMEM((n_pages * page, ncol), F32),
        ],
    )
    return pl.pallas_call(
        functools.partial(_sample_attn_kernel, n_pages=n_pages, lam_init=lam_init),
        grid_spec=grid_spec,
        out_shape=jax.ShapeDtypeStruct((n, t_new, A_WIDTH), F32),
        compiler_params=_cparams("arbitrary"),
        name="sample_attn",
    )(page_table, q3, k3, v3, *([cache_k] * n_pages), *([cache_v] * n_pages),
      bias_past, bias_new, *lam_p, gsub)


def _cross_kernel(q_ref, mk_ref, mv_ref, o_ref):
    for h in range(N_XHEADS):
        sl = slice(h * XDH, (h + 1) * XDH)
        q = q_ref[:, :, sl].astype(BF16)
        k = mk_ref[:, :, sl].astype(BF16)
        v = mv_ref[:, :, sl].astype(BF16)
        s = jnp.einsum("ntd,nmd->ntm", q, k, preferred_element_type=F32)
        e = jnp.exp(s - jnp.max(s, axis=-1, keepdims=True))
        p = (e / jnp.sum(e, axis=-1, keepdims=True)).astype(BF16)
        o = jnp.einsum("ntm,nmd->ntd", p, v, preferred_element_type=F32)
        o_ref[:, :, sl] = o.astype(o_ref.dtype)


def _cross_attn(qc3, mk3, mv3, nb, ts, out_dtype):
    n, t, _ = qc3.shape
    n_mem = mk3.shape[1]
    return pl.pallas_call(
        _cross_kernel,
        grid=(n // nb, t // ts),
        in_specs=[
            pl.BlockSpec((nb, ts, X_WIDTH), lambda i, j: (i, j, 0)),
            pl.BlockSpec((nb, n_mem, X_WIDTH), lambda i, j: (i, 0, 0)),
            pl.BlockSpec((nb, n_mem, X_WIDTH), lambda i, j: (i, 0, 0)),
        ],
        out_specs=pl.BlockSpec((nb, ts, X_WIDTH), lambda i, j: (i, j, 0)),
        out_shape=jax.ShapeDtypeStruct((n, t, X_WIDTH), out_dtype),
        compiler_params=_cparams("parallel", "arbitrary"),
        name="cross_attn",
    )(qc3, mk3, mv3)


def _merge_kernel(x_ref, yc_ref, ha_ref, cx_ref, g0_ref, g1_ref, g2_ref, wc_ref, wa_ref, wx_ref, wo_ref,
                  gpost_ref, o_ref, acc_ref):
    j = pl.program_id(1)

    @pl.when(j == 0)
    def _():
        acc_ref[...] = jnp.zeros(acc_ref.shape, F32)

    def branch(act_ref, w_ref, gate_ref):
        y = jnp.dot(act_ref[...].astype(BF16), w_ref[...], preferred_element_type=F32)
        return gate_ref[...].astype(F32) * y

    merged = branch(yc_ref, wc_ref, g0_ref) + branch(ha_ref, wa_ref, g1_ref) + branch(cx_ref, wx_ref, g2_ref)
    acc_ref[...] += jnp.dot(merged.astype(BF16), wo_ref[...], preferred_element_type=F32)

    @pl.when(j == pl.num_programs(1) - 1)
    def _():
        o_ref[...] = x_ref[...] + _rms(acc_ref[...], gpost_ref[...])


def _merge(x2d, yc, ha, cx, gates, wc, wa, wx, wo, gpost, tm, tn):
    m = x2d.shape[0]
    nj = D_MODEL // tn
    act = lambda w: pl.BlockSpec((tm, w), lambda i, j: (i, 0))
    gate = lambda b: pl.BlockSpec((tm, tn), lambda i, j: (i, b * nj + j))
    wcol = lambda k: pl.BlockSpec((k, tn), lambda i, j: (0, j))
    return pl.pallas_call(
        _merge_kernel,
        grid=(m // tm, nj),
        in_specs=[
            pl.BlockSpec((tm, D_MODEL), lambda i, j: (i, 0)),
            act(C_CONV), act(A_WIDTH), act(X_WIDTH),
            gate(0), gate(1), gate(2),
            wcol(C_CONV), wcol(A_WIDTH), wcol(X_WIDTH),
            pl.BlockSpec((tn, D_MODEL), lambda i, j: (j, 0)),
            pl.BlockSpec((1, D_MODEL), lambda i, j: (0, 0)),
        ],
        out_specs=pl.BlockSpec((tm, D_MODEL), lambda i, j: (i, 0)),
        out_shape=jax.ShapeDtypeStruct((m, D_MODEL), F32),
        scratch_shapes=[pltpu.VMEM((tm, D_MODEL), F32)],
        compiler_params=_cparams("parallel", "arbitrary"),
        name="merge",
    )(x2d, yc, ha, cx, gates, gates, gates, wc, wa, wx, wo, gpost)


def _ffn_kernel(x_ref, gpre_ref, wup_ref, wdn_ref, gpost_ref, o_ref, h_ref, acc_ref):
    j = pl.program_id(1)

    @pl.when(j == 0)
    def _():
        h_ref[...] = _rms(x_ref[...], gpre_ref[...]).astype(BF16)
        acc_ref[...] = jnp.zeros(acc_ref.shape, F32)

    u = jnp.dot(h_ref[...], wup_ref[...], preferred_element_type=F32)
    u = jnp.square(jnp.maximum(u, 0.0))
    acc_ref[...] += jnp.dot(u.astype(BF16), wdn_ref[...], preferred_element_type=F32)

    @pl.when(j == pl.num_programs(1) - 1)
    def _():
        o_ref[...] = x_ref[...] + _rms(acc_ref[...], gpost_ref[...])


def _ffn(x2d, gpre, wup, wdn, gpost, tm, tf):
    m = x2d.shape[0]
    d_ff = wup.shape[1]
    vec = pl.BlockSpec((1, D_MODEL), lambda i, j: (0, 0))
    return pl.pallas_call(
        _ffn_kernel,
        grid=(m // tm, d_ff // tf),
        in_specs=[
            pl.BlockSpec((tm, D_MODEL), lambda i, j: (i, 0)),
            vec,
            pl.BlockSpec((D_MODEL, tf), lambda i, j: (0, j)),
            pl.BlockSpec((tf, D_MODEL), lambda i, j: (j, 0)),
            vec,
        ],
        out_specs=pl.BlockSpec((tm, D_MODEL), lambda i, j: (i, 0)),
        out_shape=jax.ShapeDtypeStruct((m, D_MODEL), F32),
        scratch_shapes=[pltpu.VMEM((tm, D_MODEL), BF16), pltpu.VMEM((tm, D_MODEL), F32)],
        compiler_params=_cparams("parallel", "arbitrary"),
        name="ffn",
    )(x2d, gpre, wup, wdn, gpost)


def _row(v):
    return v.reshape(1, -1)


def kernel(x_prompt, x_sample, mem_prompt, cache_k, cache_v, cache_conv, cache_mem_k, cache_mem_v, page_table, rel_table, g_mix_pre, w_in, conv_dw, conv_db, conv_ln_g, conv_ln_b, w_conv_out, lambda_q1, lambda_k1, lambda_q2, lambda_k2, g_attn_sub, w_attn_out, g_mem, w_mem_k, w_mem_v, w_cross_out, w_o, g_mix_post, g_ffn_pre, w_up, w_down, g_ffn_post):
    depth = w_in.shape[0]
    b, s, _ = x_prompt.shape
    n, t_new, _ = x_sample.shape
    n_mem = mem_prompt.shape[1]
    n_pool, page = cache_k.shape[1], cache_k.shape[2]
    n_pre = CONV_W - 1

    xp = x_prompt.reshape(b * s, D_MODEL)
    xs = x_sample.reshape(n * t_new, D_MODEL)
    bias_prompt = _prompt_bias(rel_table, T_ATT)
    bias_past, bias_new = _sample_bias(rel_table, page, t_new)
    tm_p = min(TM_ROWS, b * s)
    tm_s = min(TM_ROWS, n * t_new)

    outs = [[] for _ in range(8)]
    for l in range(depth):
        lam_init = 0.8 - 0.6 * math.exp(-0.3 * l)
        lam_p = (_row(lambda_q1[l]), _row(lambda_k1[l]), _row(lambda_q2[l]), _row(lambda_k2[l]))
        gsub = _row(g_attn_sub[l])
        w_in_b = _permute_w_in(w_in[l])
        wc, wa, wx = w_conv_out[l].astype(BF16), w_attn_out[l].astype(BF16), w_cross_out[l].astype(BF16)
        wo, wup, wdn = w_o[l].astype(BF16), w_up[l].astype(BF16), w_down[l].astype(BF16)
        conv_p = (conv_dw[l], _row(conv_db[l]), _row(conv_ln_g[l]), _row(conv_ln_b[l]))

        def tail(x2d, yc, ha, cx, gates, tm):
            x1 = _merge(x2d, yc, ha, cx, gates, wc, wa, wx, wo, _row(g_mix_post[l]), tm, TN_IN)
            return _ffn(x1, _row(g_ffn_pre[l]), wup, wdn, _row(g_ffn_post[l]), tm, TF_FFN)

        a, q, k, kb, v, vb, qc, gates = _inproj(xp, _row(g_mix_pre[l]), w_in_b, tm_p)
        a3 = a.reshape(b, s, C_CONV)
        yc = _conv_prompt(a3, *conv_p).reshape(b * s, C_CONV)
        ha = _flash_prompt(q.reshape(b, s, A_WIDTH), kb.reshape(b, s, A_WIDTH), vb.reshape(b, s, A_WIDTH),
                           bias_prompt, lam_p, gsub, lam_init).reshape(b * s, A_WIDTH)
        mem2d = mem_prompt.reshape(b * n_mem, D_MODEL)
        tm_m = min(TM_ROWS, b * n_mem)
        mk = _norm_matmul(mem2d, _row(g_mem[l]), w_mem_k[l].astype(BF16), tm_m, TN_IN)
        mv = _norm_matmul(mem2d, _row(g_mem[l]), w_mem_v[l].astype(BF16), tm_m, TN_IN)
        mk3, mv3 = mk.reshape(b, n_mem, X_WIDTH), mv.reshape(b, n_mem, X_WIDTH)
        cx = _cross_attn(qc.reshape(b, s, X_WIDTH), mk3, mv3, 1, TS_CROSS, BF16).reshape(b * s, X_WIDTH)
        xp = tail(xp, yc, ha, cx, gates, tm_p)
        outs[0].append(k.reshape(b, s, N_HEADS, DV))
        outs[1].append(v.reshape(b, s, N_HEADS, DV))
        outs[2].append(a3[:, s - n_pre:])
        outs[3].append(mk3.reshape(b, n_mem, N_XHEADS, XDH))
        outs[4].append(mv3.reshape(b, n_mem, N_XHEADS, XDH))

        a, q, k, kb, v, vb, qc, gates = _inproj(xs, _row(g_mix_pre[l]), w_in_b, tm_s)
        a3 = a.reshape(n, t_new, C_CONV)
        yc = _conv_sample(cache_conv[l], a3, *conv_p).reshape(n * t_new, C_CONV)
        ck = cache_k[l].reshape(n_pool, page, A_WIDTH)
        cv = cache_v[l].reshape(n_pool, page, A_WIDTH)
        ha = _sample_attn(page_table, q.reshape(n, t_new, A_WIDTH), k.reshape(n, t_new, A_WIDTH),
                          v.reshape(n, t_new, A_WIDTH), ck, cv, bias_past, bias_new, lam_p, gsub,
                          lam_init).reshape(n * t_new, A_WIDTH)
        cmk = cache_mem_k[l].reshape(n, n_mem, X_WIDTH)
        cmv = cache_mem_v[l].reshape(n, n_mem, X_WIDTH)
        cx = _cross_attn(qc.reshape(n, t_new, X_WIDTH), cmk, cmv, NB_SAMPLE, t_new, F32)
        xs = tail(xs, yc, ha, cx.reshape(n * t_new, X_WIDTH), gates, tm_s)
        outs[5].append(k.reshape(n, t_new, N_HEADS, DV))
        outs[6].append(v.reshape(n, t_new, N_HEADS, DV))
        outs[7].append(jnp.concatenate([cache_conv[l], a3], axis=1)[:, t_new:])

    stacked = [jnp.stack(o) for o in outs]
    return (xp.reshape(b, s, D_MODEL), xs.reshape(n, t_new, D_MODEL), *stacked)
```
